```python
import jax, jax.numpy as jnp
from jax import lax
import numpy as np

D_MODEL = 1024
BATCH = 4
SEQ = 4096
DEPTH = 4

N_MIXERS = 2
ALPHA = (2 * DEPTH) ** 0.25
BETA = (8 * DEPTH) ** -0.25
LN_EPS = 1e-5
N_A = (DEPTH + 1) // 2
N_B = DEPTH // 2
D_FF = 2816
ML_HEADS = 4
ML_DQK = D_MODEL // (2 * ML_HEADS)
ML_DV = D_MODEL // ML_HEADS
ML_QK = ML_HEADS * ML_DQK
ML_PROJ = 2 * ML_QK + 2 * D_MODEL + 4 * ML_HEADS
CHUNK = 64
HEAD_DIM = 64
N_Q_HEADS = D_MODEL // HEAD_DIM
N_KV_HEADS = 4
GROUP = N_Q_HEADS // N_KV_HEADS
WINDOW = 128
BLOCK = 128
AT_PROJ = (N_Q_HEADS + 2 * N_KV_HEADS) * HEAD_DIM
MASK_VALUE = -1e30

kernel_name = "hybrid_mlstm_swa_macaron_deepnorm"


def layer_norm(x, g, b):
    xf = x.astype(jnp.float32)
    mu = jnp.mean(xf, axis=-1, keepdims=True)
    var = jnp.mean(jnp.square(xf - mu), axis=-1, keepdims=True)
    return ((xf - mu) * lax.rsqrt(var + LN_EPS) * g + b).astype(x.dtype)


def swiglu(x, w_in, w_out):
    gu = x @ w_in
    g, u = gu[..., :D_FF], gu[..., D_FF:]
    return (jax.nn.silu(g) * u) @ w_out


def mlstm_chunkwise(q, k, v, i_pre, f_pre):
    B, H, S, dk = q.shape
    dv = v.shape[-1]
    nc = S // CHUNK
    q = q.astype(jnp.float32).reshape(B, H, nc, CHUNK, dk)
    k = k.astype(jnp.float32).reshape(B, H, nc, CHUNK, dk)
    v = v.astype(jnp.float32).reshape(B, H, nc, CHUNK, dv)
    logf = jax.nn.log_sigmoid(f_pre.astype(jnp.float32)).reshape(B, H, nc, CHUNK)
    ig = i_pre.astype(jnp.float32).reshape(B, H, nc, CHUNK)
    g = jnp.cumsum(logf, axis=-1)
    G = g[..., -1]
    w_end = G[..., None] - g + ig
    a = jnp.max(w_end, axis=-1)
    e_end = jnp.exp(w_end - a[..., None])
    K_c = jnp.einsum('bhcl,bhclv,bhclk->bhcvk', e_end, v, k)
    N_c = jnp.einsum('bhcl,bhclk->bhck', e_end, k)

    def step(carry, inp):
        C, n, m = carry
        Gc, ac, Kc, Nc = inp
        m_new = jnp.maximum(Gc + m, ac)
        sp = jnp.exp(Gc + m - m_new)
        sc = jnp.exp(ac - m_new)
        C_new = sp[..., None, None] * C + sc[..., None, None] * Kc
        n_new = sp[..., None] * n + sc[..., None] * Nc
        return (C_new, n_new, m_new), (C, n, m)

    init = (jnp.zeros((B, H, dv, dk), jnp.float32), jnp.zeros((B, H, dk), jnp.float32),
            jnp.zeros((B, H), jnp.float32))
    xs = (jnp.moveaxis(G, 2, 0), jnp.moveaxis(a, 2, 0), jnp.moveaxis(K_c, 2, 0), jnp.moveaxis(N_c, 2, 0))
    _, (C0, n0, m0) = lax.scan(step, init, xs)
    C0 = jnp.moveaxis(C0, 0, 2)
    n0 = jnp.moveaxis(n0, 0, 2)
    m0 = jnp.moveaxis(m0, 0, 2)

    tril = jnp.tril(jnp.ones((CHUNK, CHUNK), dtype=bool))
    dmat = g[..., :, None] - g[..., None, :] + ig[..., None, :]
    dmat = jnp.where(tril, dmat, MASK_VALUE)
    inter_log = g + m0[..., None]
    m_out = jnp.maximum(inter_log, jnp.max(dmat, axis=-1))
    wts = jnp.exp(dmat - m_out[..., None])
    s_inter = jnp.exp(inter_log - m_out)
    s_qk = jnp.einsum('bhcjd,bhcsd->bhcjs', q, k) * wts
    num = jnp.einsum('bhcjs,bhcsv->bhcjv', s_qk, v) + s_inter[..., None] * jnp.einsum('bhcvd,bhcjd->bhcjv', C0, q)
    den = jnp.sum(s_qk, axis=-1) + s_inter * jnp.einsum('bhcd,bhcjd->bhcj', n0, q)
    den = jnp.maximum(jnp.abs(den), jnp.exp(-m_out))
    h = num / den[..., None]
    return h.reshape(B, H, S, dv)


def mlstm_mixer(x, w_in, gate_b, norm_g, w_out):
    B, S, _ = x.shape
    p = x @ w_in
    q = p[..., :ML_QK].reshape(B, S, ML_HEADS, ML_DQK).transpose(0, 2, 1, 3)
    k = p[..., ML_QK:2 * ML_QK].reshape(B, S, ML_HEADS, ML_DQK).transpose(0, 2, 1, 3) * (ML_DQK ** -0.5)
    v = p[..., 2 * ML_QK:2 * ML_QK + D_MODEL].reshape(B, S, ML_HEADS, ML_DV).transpose(0, 2, 1, 3)
    o = p[..., 2 * ML_QK + D_MODEL:2 * ML_QK + 2 * D_MODEL]
    gates = (p[..., 2 * ML_QK + 2 * D_MODEL:] + gate_b).reshape(B, S, 4, ML_HEADS).transpose(2, 0, 3, 1)
    h_f = mlstm_chunkwise(q, k, v, gates[0], gates[1])
    flip = lambda a: jnp.flip(a, axis=2)
    h_b = flip(mlstm_chunkwise(flip(q), flip(k), flip(v), flip(gates[2]), flip(gates[3])))
    h = h_f + h_b
    mu = jnp.mean(h, axis=-1, keepdims=True)
    var = jnp.mean(jnp.square(h - mu), axis=-1, keepdims=True)
    h = (h - mu) * lax.rsqrt(var + LN_EPS)
    h = h.transpose(0, 2, 1, 3).reshape(B, S, D_MODEL) * norm_g
    out = jax.nn.sigmoid(o.astype(jnp.float32)) * h
    return out.astype(x.dtype) @ w_out


def alibi_slopes():
    return jnp.exp2(-8.0 * jnp.arange(1, N_Q_HEADS + 1, dtype=jnp.float32) / N_Q_HEADS)


def window_attention(x, w_in, sink, w_out):
    B, S, _ = x.shape
    nb = S // BLOCK
    p = x @ w_in
    nq = N_Q_HEADS * HEAD_DIM
    nk = N_KV_HEADS * HEAD_DIM
    q = p[..., :nq].reshape(B, nb, BLOCK, N_KV_HEADS, GROUP, HEAD_DIM)
    k = p[..., nq:nq + nk].reshape(B, S, N_KV_HEADS, HEAD_DIM)
    v = p[..., nq + nk:].reshape(B, S, N_KV_HEADS, HEAD_DIM)
    pad = ((0, 0), (BLOCK, BLOCK), (0, 0), (0, 0))
    kp = jnp.pad(k, pad).reshape(B, nb + 2, BLOCK, N_KV_HEADS, HEAD_DIM)
    vp = jnp.pad(v, pad).reshape(B, nb + 2, BLOCK, N_KV_HEADS, HEAD_DIM)
    kw = jnp.concatenate([kp[:, :-2], kp[:, 1:-1], kp[:, 2:]], axis=2)
    vw = jnp.concatenate([vp[:, :-2], vp[:, 1:-1], vp[:, 2:]], axis=2)
    s = jnp.einsum('bnqkgd,bnskd->bnkgqs', q, kw).astype(jnp.float32) * (HEAD_DIM ** -0.5)
    qi = jnp.arange(BLOCK)[:, None]
    kj = jnp.arange(3 * BLOCK)[None, :]
    rel = qi - kj + BLOCK
    dist = jnp.abs(rel).astype(jnp.float32)
    key_pos = jnp.arange(nb)[:, None] * BLOCK - BLOCK + jnp.arange(3 * BLOCK)[None, :]
    valid = (jnp.abs(rel) <= WINDOW)[None] & ((key_pos >= 0) & (key_pos < S))[:, None, :]
    slopes = alibi_slopes().reshape(N_KV_HEADS, GROUP)
    s = s - slopes[:, :, None, None] * dist
    s = jnp.where(valid[None, :, None, None], s, MASK_VALUE)
    sink_l = sink.astype(jnp.float32).reshape(N_KV_HEADS, GROUP)[None, None, :, :, None, None]
    m = jnp.maximum(jnp.max(s, axis=-1, keepdims=True), sink_l)
    pe = jnp.exp(s - m)
    attn = pe / (jnp.sum(pe, axis=-1, keepdims=True) + jnp.exp(sink_l - m))
    o = jnp.einsum('bnkgqs,bnskd->bnqkgd', attn, vw.astype(jnp.float32)).reshape(B, S, nq)
    return o.astype(x.dtype) @ w_out


def setup_inputs(seed: int = 0) -> dict:
    key = jax.random.key(seed)
    ks = jax.random.split(key, 16)
    nrm = jax.random.normal
    x = nrm(ks[0], (BATCH, SEQ, D_MODEL), jnp.float32)
    ffn_w_in = nrm(ks[1], (DEPTH, 2, D_MODEL, 2 * D_FF), jnp.float32) * D_MODEL ** -0.5
    ffn_w_out = nrm(ks[2], (DEPTH, 2, D_FF, D_MODEL), jnp.float32) * (D_FF ** -0.5 * BETA)
    ln_g = 1.0 + 0.02 * nrm(ks[3], (DEPTH, 3, D_MODEL), jnp.float32)
    ln_b = 0.02 * nrm(ks[4], (DEPTH, 3, D_MODEL), jnp.float32)
    ml_w_in = nrm(ks[5], (N_A, D_MODEL, ML_PROJ), jnp.float32) * D_MODEL ** -0.5
    ml_w_in = ml_w_in.at[..., 2 * ML_QK:2 * ML_QK + D_MODEL].multiply(BETA)
    f_off = jnp.linspace(3.0, 6.0, ML_HEADS, dtype=jnp.float32)
    zero = jnp.zeros((ML_HEADS,), jnp.float32)
    gate_off = jnp.stack([zero, f_off, zero, f_off])
    ml_gate_b = (0.1 * nrm(ks[6], (N_A, 4, ML_HEADS), jnp.float32) + gate_off).reshape(N_A, 4 * ML_HEADS)
    ml_norm_g = 1.0 + 0.02 * nrm(ks[7], (N_A, D_MODEL), jnp.float32)
    ml_w_out = nrm(ks[8], (N_A, D_MODEL, D_MODEL), jnp.float32) * (D_MODEL ** -0.5 * BETA)
    at_w_in = nrm(ks[9], (N_B, D_MODEL, AT_PROJ), jnp.float32) * D_MODEL ** -0.5
    at_w_in = at_w_in.at[..., (N_Q_HEADS + N_KV_HEADS) * HEAD_DIM:].multiply(BETA)
    at_sink = 0.5 * nrm(ks[10], (N_B, N_Q_HEADS), jnp.float32)
    at_w_out = nrm(ks[11], (N_B, D_MODEL, D_MODEL), jnp.float32) * (D_MODEL ** -0.5 * BETA)
    return {"x": x, "ffn_w_in": ffn_w_in, "ffn_w_out": ffn_w_out, "ln_g": ln_g, "ln_b": ln_b,
            "ml_w_in": ml_w_in, "ml_gate_b": ml_gate_b, "ml_norm_g": ml_norm_g, "ml_w_out": ml_w_out,
            "at_w_in": at_w_in, "at_sink": at_sink, "at_w_out": at_w_out}


def reference(x, ffn_w_in, ffn_w_out, ln_g, ln_b, ml_w_in, ml_gate_b, ml_norm_g, ml_w_out,
              at_w_in, at_sink, at_w_out):
    for l in range(DEPTH):
        x = layer_norm(ALPHA * x + 0.5 * swiglu(x, ffn_w_in[l, 0], ffn_w_out[l, 0]), ln_g[l, 0], ln_b[l, 0])
        j = l // N_MIXERS
        if l % N_MIXERS == 0:
            y = mlstm_mixer(x, ml_w_in[j], ml_gate_b[j], ml_norm_g[j], ml_w_out[j])
        else:
            y = window_attention(x, at_w_in[j], at_sink[j], at_w_out[j])
        x = layer_norm(ALPHA * x + y, ln_g[l, 1], ln_b[l, 1])
        x = layer_norm(ALPHA * x + 0.5 * swiglu(x, ffn_w_in[l, 1], ffn_w_out[l, 1]), ln_g[l, 2], ln_b[l, 2])
    return x
```

```python
import functools

import numpy as np
import jax
import jax.numpy as jnp
from jax import lax
from jax.experimental import pallas as pl
from jax.experimental.pallas import tpu as pltpu

D_MODEL = 1024
DEPTH = 4
ALPHA = (2 * DEPTH) ** 0.25
LN_EPS = 1e-5
D_FF = 2816
ML_HEADS = 4
ML_DQK = 128
ML_DV = 256
ML_QK = ML_HEADS * ML_DQK
HEAD_DIM = 64
N_Q_HEADS = 16
N_KV_HEADS = 4
GROUP = N_Q_HEADS // N_KV_HEADS
WINDOW = 128
BLOCK = 128
MASK_VALUE = -1e30

LANES = 128
VMEM_LIMIT_BYTES = 56 * 1024 * 1024

ROW_TILE = 512
FF_CHUNK = 256
ML_CHUNK = 128
GATE_LANES = 128

BF16 = jnp.bfloat16
F32 = jnp.float32
NT_DIMS = (((1,), (1,)), ((), ()))
TN_DIMS = (((0,), (0,)), ((), ()))


def _layer_norm(z, g, b):
    mu = jnp.mean(z, axis=-1, keepdims=True)
    zc = z - mu
    var = jnp.mean(zc * zc, axis=-1, keepdims=True)
    return zc * lax.rsqrt(var + LN_EPS) * g + b


def _resident(shape):
    nd = len(shape)
    return pl.BlockSpec(shape, lambda *_: (0,) * nd, pipeline_mode=pl.Buffered(1))


def _params(n_axes):
    return pltpu.CompilerParams(dimension_semantics=("arbitrary",) * n_axes,
                                vmem_limit_bytes=VMEM_LIMIT_BYTES)


def _ffn_ln_kernel(x_ref, wgu_ref, wo_ref, g_ref, b_ref, o_ref, xb_ref, h_ref):
    x = x_ref[...]
    xb_ref[...] = x.astype(BF16)
    for c in range(D_FF // FF_CHUNK):
        gu = jnp.dot(xb_ref[...], wgu_ref[:, 2 * c * FF_CHUNK:2 * (c + 1) * FF_CHUNK],
                     preferred_element_type=F32)
        g = gu[:, :FF_CHUNK]
        u = gu[:, FF_CHUNK:]
        h_ref[:, c * FF_CHUNK:(c + 1) * FF_CHUNK] = (g * jax.nn.sigmoid(g) * u).astype(BF16)
    y = jnp.dot(h_ref[...], wo_ref[...], preferred_element_type=F32)
    o_ref[...] = _layer_norm(ALPHA * x + 0.5 * y, g_ref[...], b_ref[...])


def _ffn_ln(x2, wgu, wo, g, b):
    m = x2.shape[0]
    tm = ROW_TILE
    row = pl.BlockSpec((tm, D_MODEL), lambda i: (i, 0))
    return pl.pallas_call(
        _ffn_ln_kernel,
        grid=(m // tm,),
        in_specs=[row, _resident(wgu.shape), _resident(wo.shape),
                  _resident((1, D_MODEL)), _resident((1, D_MODEL))],
        out_specs=row,
        out_shape=jax.ShapeDtypeStruct((m, D_MODEL), F32),
        scratch_shapes=[pltpu.VMEM((tm, D_MODEL), BF16), pltpu.VMEM((tm, D_FF), BF16)],
        compiler_params=_params(1),
        name="ffn_ln",
    )(x2, wgu, wo, g.reshape(1, D_MODEL), b.reshape(1, D_MODEL))


def _log_sigmoid(x):
    return jnp.minimum(x, 0.0) - jnp.log1p(jnp.exp(-jnp.abs(x)))


def _split3(a):
    hi = a.astype(BF16)
    r = a - hi.astype(F32)
    mid = r.astype(BF16)
    lo = (r - mid.astype(F32)).astype(BF16)
    return hi, mid, lo


def _ml_proj_kernel(x_ref, wq_ref, wk_ref, wv_ref, wgh_ref, wgl_ref, gb_ref, tl_ref, tu_ref,
                    q_ref, k_ref, v_ref, gcol_ref, grow_ref):
    x = x_ref[...]
    xb = x.astype(BF16)
    q_ref[...] = jnp.dot(xb, wq_ref[...], preferred_element_type=F32).astype(BF16)
    k_ref[...] = (jnp.dot(xb, wk_ref[...], preferred_element_type=F32) * (ML_DQK ** -0.5)).astype(BF16)
    v_ref[...] = jnp.dot(xb, wv_ref[...], preferred_element_type=F32).astype(BF16)

    xl = (x - xb.astype(F32)).astype(BF16)
    pre = (jnp.dot(xb, wgh_ref[...], preferred_element_type=F32)
           + jnp.dot(xl, wgh_ref[...], preferred_element_type=F32)
           + jnp.dot(xb, wgl_ref[...], preferred_element_type=F32)) + gb_ref[...]
    logf = _log_sigmoid(pre)
    hi, mid, lo = _split3(logf)
    tl = tl_ref[...]
    tu = tu_ref[...]
    gp = (jnp.dot(tl, hi, preferred_element_type=F32) + jnp.dot(tl, mid, preferred_element_type=F32)
          + jnp.dot(tl, lo, preferred_element_type=F32))
    gs = (jnp.dot(tu, hi, preferred_element_type=F32) + jnp.dot(tu, mid, preferred_element_type=F32)
          + jnp.dot(tu, lo, preferred_element_type=F32))
    lane = lax.broadcasted_iota(jnp.int32, pre.shape, 1)
    g = jnp.where(lane % 4 < 2, gp, gs)
    ig = pltpu.roll(pre, GATE_LANES - 16, 1)
    out = jnp.where(lane % 2 == 0, g, ig - g)
    gcol_ref[...] = out
    grow_ref[...] = jnp.transpose(out)[:4 * ML_HEADS, :]


def _chunk_sum_matrices(tm, chunk):
    r = np.arange(tm)
    same = (r[:, None] // chunk) == (r[None, :] // chunk)
    tl = same & (r[None, :] <= r[:, None])
    tu = same & (r[None, :] >= r[:, None])
    return jnp.asarray(tl, BF16), jnp.asarray(tu, BF16)


def _ml_proj(x2, batch, seq, wq, wk, wv, wgh, wgl, gb):
    m = x2.shape[0]
    tm = ROW_TILE
    nts = seq // tm
    tl, tu = _chunk_sum_matrices(tm, ML_CHUNK)
    row = lambda n: pl.BlockSpec((tm, n), lambda i: (i, 0))
    return pl.pallas_call(
        _ml_proj_kernel,
        grid=(m // tm,),
        in_specs=[row(D_MODEL), _resident(wq.shape), _resident(wk.shape), _resident(wv.shape),
                  _resident(wgh.shape), _resident(wgl.shape), _resident(gb.shape),
                  _resident(tl.shape), _resident(tu.shape)],
        out_specs=[row(ML_QK), row(ML_QK), row(D_MODEL), row(GATE_LANES),
                   pl.BlockSpec((None, 4 * ML_HEADS, tm), lambda i: (i // nts, 0, i % nts))],
        out_shape=[jax.ShapeDtypeStruct((m, ML_QK), BF16), jax.ShapeDtypeStruct((m, ML_QK), BF16),
                   jax.ShapeDtypeStruct((m, D_MODEL), BF16), jax.ShapeDtypeStruct((m, GATE_LANES), F32),
                   jax.ShapeDtypeStruct((batch, 4 * ML_HEADS, seq), F32)],
        compiler_params=_params(1),
        name="ml_proj",
    )(x2, wq, wk, wv, wgh, wgl, gb, tl, tu)


def _ml_core_kernel(q_ref, k_ref, v_ref, grow_ref, gcol_ref, o_ref, hf_ref, hb_ref, ct_ref, n_ref, m_ref):
    seq = q_ref.shape[0]
    L = ML_CHUNK
    nc = seq // L
    ct_ref[...] = jnp.zeros_like(ct_ref)
    n_ref[...] = jnp.zeros_like(n_ref)
    m_ref[...] = jnp.zeros_like(m_ref)
    row_i = lax.broadcasted_iota(jnp.int32, (L, L), 0)
    col_i = lax.broadcasted_iota(jnp.int32, (L, L), 1)
    masks = (col_i <= row_i, col_i >= row_i)

    def step(c, d, h_ref):
        r0 = pl.multiple_of(c * L, L)
        qc = q_ref[pl.ds(r0, L), :]
        kc = k_ref[pl.ds(r0, L), :]
        vc = v_ref[pl.ds(r0, L), :]
        g_row = grow_ref[2 * d:2 * d + 1, pl.ds(r0, L)]
        b_row = grow_ref[2 * d + 1:2 * d + 2, pl.ds(r0, L)]
        g_col = gcol_ref[pl.ds(r0, L), 2 * d:2 * d + 1]
        b_col = gcol_ref[pl.ds(r0, L), 2 * d + 1:2 * d + 2]
        m0 = m_ref[d]
        ct = ct_ref[d]
        nrow = n_ref[d]

        bmat = jnp.where(masks[d], b_row, MASK_VALUE)
        bmax = jnp.max(bmat, axis=-1, keepdims=True)
        mj = jnp.maximum(m0, bmax)
        wts = jnp.exp(bmat - mj)
        s_int = jnp.exp(m0 - mj)
        a = lax.dot_general(qc, kc, NT_DIMS, preferred_element_type=F32) * wts
        num = (jnp.dot(a.astype(BF16), vc, preferred_element_type=F32)
               + s_int * jnp.dot(qc, ct.astype(BF16), preferred_element_type=F32))
        den = (jnp.sum(a, axis=-1, keepdims=True)
               + s_int * jnp.sum(qc.astype(F32) * nrow, axis=-1, keepdims=True))
        den = jnp.maximum(jnp.abs(den), jnp.exp(-(g_col + mj)))
        h_ref[pl.ds(r0, L), :] = num * (1.0 / den)

        a_loc = jnp.max(b_row, axis=-1, keepdims=True)
        g_tot = g_row[:, L - 1:L] if d == 0 else g_row[:, 0:1]
        e_end = jnp.exp(b_col - a_loc)
        a_c = g_tot + a_loc
        m_new = jnp.maximum(g_tot + m0, a_c)
        sp = jnp.exp(g_tot + m0 - m_new)
        sc = jnp.exp(a_c - m_new)
        ve = (vc.astype(F32) * e_end).astype(BF16)
        kv = lax.dot_general(kc, ve, TN_DIMS, preferred_element_type=F32)
        ct_ref[d] = sp * ct + sc * kv
        n_ref[d] = sp * nrow + sc * jnp.sum(kc.astype(F32) * e_end, axis=0, keepdims=True)
        m_ref[d] = m_new

    def body(i, carry):
        step(i, 0, hf_ref)
        step(nc - 1 - i, 1, hb_ref)
        return carry

    lax.fori_loop(0, nc, body, 0)

    def norm(i, carry):
        r0 = pl.multiple_of(i * L, L)
        h = hf_ref[pl.ds(r0, L), :] + hb_ref[pl.ds(r0, L), :]
        mu = jnp.mean(h, axis=-1, keepdims=True)
        hc = h - mu
        var = jnp.mean(hc * hc, axis=-1, keepdims=True)
        o_ref[pl.ds(r0, L), :] = hc * lax.rsqrt(var + LN_EPS)
        return carry

    lax.fori_loop(0, nc, norm, 0)


def _ml_core(q, k, v, grow, gcol):
    batch, seq, _ = q.shape
    return pl.pallas_call(
        _ml_core_kernel,
        grid=(batch, ML_HEADS),
        in_specs=[pl.BlockSpec((None, seq, ML_DQK), lambda b, h: (b, 0, h)),
                  pl.BlockSpec((None, seq, ML_DQK), lambda b, h: (b, 0, h)),
                  pl.BlockSpec((None, seq, ML_DV), lambda b, h: (b, 0, h)),
                  pl.BlockSpec((None, None, 4, seq), lambda b, h: (b, h, 0, 0)),
                  pl.BlockSpec((None, None, seq, 4), lambda b, h: (b, h, 0, 0))],
        out_specs=pl.BlockSpec((None, seq, ML_DV), lambda b, h: (b, 0, h)),
        out_shape=jax.ShapeDtypeStruct((batch, seq, D_MODEL), F32),
        scratch_shapes=[pltpu.VMEM((seq, ML_DV), F32), pltpu.VMEM((seq, ML_DV), F32),
                        pltpu.VMEM((2, ML_DQK, ML_DV), F32), pltpu.VMEM((2, 1, ML_DQK), F32),
                        pltpu.VMEM((2, 1, 1), F32)],
        compiler_params=_params(2),
        name="ml_core",
    )(q, k, v, grow, gcol)


def _ml_out_kernel(x_ref, hn_ref, wog_ref, ng_ref, wo_ref, g_ref, b_ref, o_ref):
    x = x_ref[...]
    og = jnp.dot(x.astype(BF16), wog_ref[...], preferred_element_type=F32)
    gated = jax.nn.sigmoid(og) * (hn_ref[...] * ng_ref[...])
    y = jnp.dot(gated.astype(BF16), wo_ref[...], preferred_element_type=F32)
    o_ref[...] = _layer_norm(ALPHA * x + y, g_ref[...], b_ref[...])


def _ml_out(x2, hn2, wog, norm_g, wo, g, b):
    m = x2.shape[0]
    tm = ROW_TILE
    row = pl.BlockSpec((tm, D_MODEL), lambda i: (i, 0))
    vec = _resident((1, D_MODEL))
    return pl.pallas_call(
        _ml_out_kernel,
        grid=(m // tm,),
        in_specs=[row, row, _resident(wog.shape), vec, _resident(wo.shape), vec, vec],
        out_specs=row,
        out_shape=jax.ShapeDtypeStruct((m, D_MODEL), F32),
        compiler_params=_params(1),
        name="ml_out",
    )(x2, hn2, wog, norm_g.reshape(1, D_MODEL), wo, g.reshape(1, D_MODEL), b.reshape(1, D_MODEL))


def _at_proj_kernel(x_ref, wq_ref, wk_ref, wv_ref, q_ref, k_ref, v_ref):
    xb = x_ref[...].astype(BF16)
    q_ref[...] = (jnp.dot(xb, wq_ref[...], preferred_element_type=F32) * (HEAD_DIM ** -0.5)).astype(BF16)
    k_ref[...] = jnp.dot(xb, wk_ref[...], preferred_element_type=F32).astype(BF16)
    v_ref[...] = jnp.dot(xb, wv_ref[...], preferred_element_type=F32).astype(BF16)


def _at_proj(x2, wq, wk, wv):
    m = x2.shape[0]
    tm = ROW_TILE
    nk = N_KV_HEADS * HEAD_DIM
    row = lambda n: pl.BlockSpec((tm, n), lambda i: (i, 0))
    return pl.pallas_call(
        _at_proj_kernel,
        grid=(m // tm,),
        in_specs=[row(D_MODEL), _resident(wq.shape), _resident(wk.shape), _resident(wv.shape)],
        out_specs=[row(D_MODEL), row(nk), row(nk)],
        out_shape=[jax.ShapeDtypeStruct((m, D_MODEL), BF16), jax.ShapeDtypeStruct((m, nk), BF16),
                   jax.ShapeDtypeStruct((m, nk), BF16)],
        compiler_params=_params(1),
        name="at_proj",
    )(x2, wq, wk, wv)


def _at_core_kernel(sink_ref, q_ref, k_ref, v_ref, o_ref):
    seq = k_ref.shape[0]
    span = 3 * BLOCK
    n = pl.program_id(1)
    start = pl.multiple_of(jnp.clip(n * BLOCK - BLOCK, 0, seq - span), BLOCK)
    kw = k_ref[pl.ds(start, span), :]
    vw = v_ref[pl.ds(start, span), :]
    t = n * BLOCK + lax.broadcasted_iota(jnp.int32, (BLOCK, span), 0)
    s_pos = start + lax.broadcasted_iota(jnp.int32, (BLOCK, span), 1)
    rel = jnp.abs(t - s_pos)
    valid = rel <= WINDOW
    dist = rel.astype(F32)
    for kh in range(N_KV_HEADS):
        k_h = kw[:, kh * HEAD_DIM:(kh + 1) * HEAD_DIM]
        v_h = vw[:, kh * HEAD_DIM:(kh + 1) * HEAD_DIM]
        for g in range(GROUP):
            h = kh * GROUP + g
            slope = 2.0 ** (-8.0 * (h + 1) / N_Q_HEADS)
            sink = sink_ref[h]
            q_h = q_ref[:, h * HEAD_DIM:(h + 1) * HEAD_DIM]
            s = lax.dot_general(q_h, k_h, NT_DIMS, preferred_element_type=F32) - slope * dist
            s = jnp.where(valid, s, MASK_VALUE)
            m = jnp.maximum(jnp.max(s, axis=-1, keepdims=True), sink)
            pe = jnp.exp(s - m)
            denom = jnp.sum(pe, axis=-1, keepdims=True) + jnp.exp(sink - m)
            o = jnp.dot(pe.astype(BF16), v_h, preferred_element_type=F32) * (1.0 / denom)
            o_ref[:, h * HEAD_DIM:(h + 1) * HEAD_DIM] = o.astype(BF16)


def _at_core(q, k, v, sink):
    batch, seq, _ = q.shape
    nk = N_KV_HEADS * HEAD_DIM
    return pl.pallas_call(
        _at_core_kernel,
        grid=(batch, seq // BLOCK),
        in_specs=[pl.BlockSpec(memory_space=pltpu.SMEM),
                  pl.BlockSpec((None, BLOCK, D_MODEL), lambda b, n: (b, n, 0)),
                  pl.BlockSpec((None, seq, nk), lambda b, n: (b, 0, 0)),
                  pl.BlockSpec((None, seq, nk), lambda b, n: (b, 0, 0))],
        out_specs=pl.BlockSpec((None, BLOCK, D_MODEL), lambda b, n: (b, n, 0)),
        out_shape=jax.ShapeDtypeStruct((batch, seq, D_MODEL), BF16),
        compiler_params=_params(2),
        name="at_core",
    )(sink, q, k, v)


def _at_out_kernel(x_ref, a_ref, wo_ref, g_ref, b_ref, o_ref):
    y = jnp.dot(a_ref[...], wo_ref[...], preferred_element_type=F32)
    o_ref[...] = _layer_norm(ALPHA * x_ref[...] + y, g_ref[...], b_ref[...])


def _at_out(x2, a2, wo, g, b):
    m = x2.shape[0]
    tm = ROW_TILE
    row = pl.BlockSpec((tm, D_MODEL), lambda i: (i, 0))
    vec = _resident((1, D_MODEL))
    return pl.pallas_call(
        _at_out_kernel,
        grid=(m // tm,),
        in_specs=[row, row, _resident(wo.shape), vec, vec],
        out_specs=row,
        out_shape=jax.ShapeDtypeStruct((m, D_MODEL), F32),
        compiler_params=_params(1),
        name="at_out",
    )(x2, a2, wo, g.reshape(1, D_MODEL), b.reshape(1, D_MODEL))


def _ffn_weights(w_in, w_out):
    nch = D_FF // FF_CHUNK
    wgu = w_in.reshape(D_MODEL, 2, nch, FF_CHUNK).transpose(0, 2, 1, 3).reshape(D_MODEL, 2 * D_FF)
    return wgu.astype(BF16), w_out.astype(BF16)


def _gate_weights(w_gate, gate_b):
    f_idx = np.array([[4 + h, 4 + h, 12 + h, 12 + h] for h in range(ML_HEADS)]).reshape(-1)
    i_idx = np.array([[h, h, 8 + h, 8 + h] for h in range(ML_HEADS)]).reshape(-1)
    idx = np.concatenate([f_idx, i_idx])
    pad = GATE_LANES - idx.size
    w = jnp.pad(w_gate[:, idx], ((0, 0), (0, pad)))
    b = jnp.pad(gate_b[idx], (0, pad)).reshape(1, GATE_LANES)
    w_hi = w.astype(BF16)
    w_lo = (w - w_hi.astype(F32)).astype(BF16)
    return w_hi, w_lo, b


def _mlstm_layer(x2, batch, seq, w_in, gate_b, norm_g, w_out, ln_g, ln_b):
    wq = w_in[:, :ML_QK].astype(BF16)
    wk = w_in[:, ML_QK:2 * ML_QK].astype(BF16)
    wv = w_in[:, 2 * ML_QK:2 * ML_QK + D_MODEL].astype(BF16)
    wog = w_in[:, 2 * ML_QK + D_MODEL:2 * ML_QK + 2 * D_MODEL].astype(BF16)
    wgh, wgl, gb = _gate_weights(w_in[:, 2 * ML_QK + 2 * D_MODEL:], gate_b)
    q, k, v, gcol, grow = _ml_proj(x2, batch, seq, wq, wk, wv, wgh, wgl, gb)
    gcol = gcol[:, :4 * ML_HEADS].reshape(batch, seq, ML_HEADS, 4).transpose(0, 2, 1, 3)
    grow = grow.reshape(batch, ML_HEADS, 4, seq)
    hn = _ml_core(q.reshape(batch, seq, ML_QK), k.reshape(batch, seq, ML_QK),
                  v.reshape(batch, seq, D_MODEL), grow, gcol)
    return _ml_out(x2, hn.reshape(batch * seq, D_MODEL), wog, norm_g, w_out.astype(BF16), ln_g, ln_b)


def _attention_layer(x2, batch, seq, w_in, sink, w_out, ln_g, ln_b):
    nq = N_Q_HEADS * HEAD_DIM
    nk = N_KV_HEADS * HEAD_DIM
    wq = w_in[:, :nq].astype(BF16)
    wk = w_in[:, nq:nq + nk].astype(BF16)
    wv = w_in[:, nq + nk:].astype(BF16)
    q, k, v = _at_proj(x2, wq, wk, wv)
    a = _at_core(q.reshape(batch, seq, nq), k.reshape(batch, seq, nk), v.reshape(batch, seq, nk), sink)
    return _at_out(x2, a.reshape(batch * seq, nq), w_out.astype(BF16), ln_g, ln_b)


def kernel(x, ffn_w_in, ffn_w_out, ln_g, ln_b, ml_w_in, ml_gate_b, ml_norm_g, ml_w_out,
           at_w_in, at_sink, at_w_out):
    batch, seq, _ = x.shape
    assert seq % ROW_TILE == 0 and seq >= 3 * BLOCK and ROW_TILE % ML_CHUNK == 0
    x2 = x.reshape(batch * seq, D_MODEL)
    for l in range(DEPTH):
        wgu, wo = _ffn_weights(ffn_w_in[l, 0], ffn_w_out[l, 0])
        x2 = _ffn_ln(x2, wgu, wo, ln_g[l, 0], ln_b[l, 0])
        j = l // 2
        if l % 2 == 0:
            x2 = _mlstm_layer(x2, batch, seq, ml_w_in[j], ml_gate_b[j], ml_norm_g[j], ml_w_out[j],
                              ln_g[l, 1], ln_b[l, 1])
        else:
            x2 = _attention_layer(x2, batch, seq, at_w_in[j], at_sink[j], at_w_out[j],
                                  ln_g[l, 1], ln_b[l, 1])
        wgu, wo = _ffn_weights(ffn_w_in[l, 1], ffn_w_out[l, 1])
        x2 = _ffn_ln(x2, wgu, wo, ln_g[l, 2], ln_b[l, 2])
    return x2.reshape(batch, seq, D_MODEL)
```
